```python
import math, functools
import jax, jax.numpy as jnp
from jax import lax
import numpy as np

D_MODEL = 1024
BATCH = 8
SEQ = 8192
DEPTH = 2

CTX_LEN = 256
GRID_W = 64
EPS = 1e-6

DA_HEADS = 4
DA_QK = 64
DA_V = 2 * DA_QK
DA_WIDTH = DA_HEADS * DA_V
Q_BLOCK = 128
ROPE_BASE = 10000.0

SSM_HEADS = 8
SSM_P = 64
SSM_WIDTH = SSM_HEADS * SSM_P
SSM_GROUPS = 2
SSM_N = 64
SSM_CONV = 3
SSM_CONV_DIM = SSM_WIDTH + 2 * SSM_GROUPS * SSM_N
SSM_CHUNK = 128

GLA_HEADS = 4
GLA_DK = 64
GLA_DV = 128
GLA_WIDTH = GLA_HEADS * GLA_DV
GLA_RANK = 16
GLA_GATE_NORM = 16.0
GLA_CHUNK = 64

IN_SPLITS = (
    2 * DA_HEADS * DA_QK, 2 * DA_HEADS * DA_QK, DA_WIDTH, DA_WIDTH,
    SSM_WIDTH, SSM_WIDTH, SSM_GROUPS * SSM_N, SSM_GROUPS * SSM_N, 2 * SSM_HEADS,
    GLA_HEADS * GLA_DK, GLA_HEADS * GLA_DK, GLA_WIDTH, GLA_WIDTH, 2 * GLA_RANK,
    D_MODEL, D_MODEL, D_MODEL)
IN_WIDTH = 4 * 512 + 2 * 512 + 2 * 128 + 16 + 2 * 256 + 2 * 512 + 32 + 3 * D_MODEL

kernel_name = "hybrid_diffattn_ssd_gla_prefix_trunk"


def rmsnorm(x, w):
    xf = x.astype(jnp.float32)
    y = xf * lax.rsqrt(jnp.mean(xf * xf, axis=-1, keepdims=True) + EPS)
    return (y * w.astype(jnp.float32)).astype(x.dtype)


def group_rmsnorm(y, w, groups):
    shp = y.shape
    yg = y.reshape(*shp[:-1], groups, shp[-1] // groups)
    return rmsnorm(yg, w.reshape(groups, -1)).reshape(shp)


def split_cols(p, sizes):
    idx = np.cumsum(sizes)[:-1].tolist()
    return jnp.split(p, idx, axis=-1)


def rope_2d_tables(n, dtype):
    rows = n // GRID_W
    row = jnp.repeat(jnp.arange(rows), GRID_W)
    col = jnp.tile(jnp.arange(GRID_W), rows)
    pos = jnp.stack([row, col], axis=-1).astype(jnp.float32)
    nf = DA_QK // 4
    inv = ROPE_BASE ** (-jnp.arange(nf, dtype=jnp.float32) / nf)
    ang = jnp.broadcast_to(pos[:, :, None, None] * inv, (n, 2, 2, nf)).reshape(n, DA_QK)
    return jnp.cos(ang).astype(dtype), jnp.sin(ang).astype(dtype)


def apply_rope_2d(x, cos, sin):
    xr = x.reshape(*x.shape[:-1], 2, 2, DA_QK // 4)
    rot = jnp.stack([-xr[..., 1, :], xr[..., 0, :]], axis=-2).reshape(x.shape)
    return x * cos[:, None, :] + rot * sin[:, None, :]


def depthwise_conv_centred(x, w, b):
    y = lax.conv_general_dilated(x, w[:, None, :].astype(x.dtype), window_strides=(1,), padding='SAME',
                                 dimension_numbers=('NWC', 'WIO', 'NWC'), feature_group_count=x.shape[-1])
    return y + b.astype(x.dtype)


def diff_softmax_attend(q, k, v, lam):
    s = jnp.einsum('bqhd,bkhd->bhqk', q, k).astype(jnp.float32) * (DA_QK ** -0.5)
    pr = jax.nn.softmax(s, axis=-1)
    bsz, _, tq, tk = pr.shape
    pr = pr.reshape(bsz, DA_HEADS, 2, tq, tk)
    wts = pr[:, :, 0] - lam * pr[:, :, 1]
    return jnp.einsum('bhqk,bkhv->bqhv', wts.astype(v.dtype), v)


def da_heads(p):
    bsz, t = p[0].shape[:2]
    q = p[0].reshape(bsz, t, 2 * DA_HEADS, DA_QK)
    k = p[1].reshape(bsz, t, 2 * DA_HEADS, DA_QK)
    v = p[2].reshape(bsz, t, DA_HEADS, DA_V)
    return q, k, v


def da_post(o, z, norm_w, lam_init):
    o = rmsnorm(o, norm_w) * (1.0 - lam_init)
    return o.reshape(*o.shape[:2], DA_WIDTH) * jax.nn.silu(z)


def ssd_chunked(x, dt, bm, cm, h0, a, d_skip):
    f32 = jnp.float32
    bsz, t, nh, p = x.shape
    g, n = bm.shape[2], bm.shape[3]
    hg = nh // g
    L = SSM_CHUNK
    nc = t // L
    xf = x.astype(f32).reshape(bsz, nc, L, g, hg, p)
    dtc = dt.astype(f32).reshape(bsz, nc, L, g, hg)
    bc = bm.astype(f32).reshape(bsz, nc, L, g, n)
    cc = cm.astype(f32).reshape(bsz, nc, L, g, n)
    acs = jnp.cumsum(dtc * a.astype(f32).reshape(g, hg), axis=2)
    tri = jnp.tril(jnp.ones((L, L), dtype=bool))
    seg = acs[:, :, :, None] - acs[:, :, None, :]
    decay = jnp.exp(jnp.where(tri[:, :, None, None], seg, -jnp.inf))
    scores = jnp.einsum('bclgn,bcsgn->bclsg', cc, bc)
    wts = scores[..., None] * decay * dtc[:, :, None]
    y = jnp.einsum('bclsgh,bcsghp->bclghp', wts, xf)
    dte = jnp.exp(acs[:, :, -1:] - acs) * dtc
    states = jnp.einsum('bclgn,bclgh,bclghp->bcghpn', bc, dte, xf)
    chunk_decay = jnp.exp(acs[:, :, -1])

    def step(h, inp):
        st, dec = inp
        return dec[..., None, None] * h + st, h

    h_last, h_starts = lax.scan(step, h0, (jnp.moveaxis(states, 1, 0), jnp.moveaxis(chunk_decay, 1, 0)))
    h_starts = jnp.moveaxis(h_starts, 0, 1)
    y = y + jnp.einsum('bclgn,bcghpn,bclgh->bclghp', cc, h_starts, jnp.exp(acs))
    y = y + d_skip.astype(f32).reshape(g, hg)[:, :, None] * xf
    return y.reshape(bsz, t, nh, p).astype(x.dtype), h_last


def gla_chunked(q, k, v, g, h0):
    f32 = jnp.float32
    bsz, t, nh, dk = q.shape
    dv = v.shape[-1]
    L = GLA_CHUNK
    nc = t // L
    qf = q.astype(f32).reshape(bsz, nc, L, nh, dk) * (dk ** -0.5)
    kf = k.astype(f32).reshape(bsz, nc, L, nh, dk)
    vf = v.astype(f32).reshape(bsz, nc, L, nh, dv)
    gc = jnp.cumsum(g.astype(f32).reshape(bsz, nc, L, nh, dk), axis=2)
    q_dec = qf * jnp.exp(gc)
    k_dec = kf * jnp.exp(-gc)
    tri = jnp.tril(jnp.ones((L, L), dtype=bool))
    att = jnp.where(tri, jnp.einsum('bclhk,bcshk->bchls', q_dec, k_dec), 0.0)
    y = jnp.einsum('bchls,bcshv->bclhv', att, vf)
    g_last = gc[:, :, -1]
    k_end = kf * jnp.exp(g_last[:, :, None] - gc)
    states = jnp.einsum('bclhk,bclhv->bchkv', k_end, vf)

    def step(h, inp):
        st, dec = inp
        return dec[..., None] * h + st, h

    h_last, h_starts = lax.scan(step, h0, (jnp.moveaxis(states, 1, 0), jnp.moveaxis(jnp.exp(g_last), 1, 0)))
    h_starts = jnp.moveaxis(h_starts, 0, 1)
    y = y + jnp.einsum('bclhk,bchkv->bclhv', q_dec, h_starts)
    return y.reshape(bsz, t, nh, dv).astype(v.dtype), h_last


def scan_ctx_then_latent(scan, ctx_in, lat_in, h0, reverse):
    def flip(arrs):
        return tuple(jnp.flip(a, axis=1) for a in arrs)
    if reverse:
        ctx_in, lat_in = flip(ctx_in), flip(lat_in)
    y_ctx, h_ctx = scan(*ctx_in, h0)
    y_lat, _ = scan(*lat_in, h_ctx)
    if reverse:
        y_ctx, y_lat = jnp.flip(y_ctx, axis=1), jnp.flip(y_lat, axis=1)
    return y_ctx, y_lat


def ssm_prep(p, conv_w, conv_b, dt_bias):
    xbc = jax.nn.silu(depthwise_conv_centred(jnp.concatenate([p[4], p[6], p[7]], axis=-1), conv_w, conv_b))
    xs, bm, cm = jnp.split(xbc, [SSM_WIDTH, SSM_WIDTH + SSM_GROUPS * SSM_N], axis=-1)
    bsz, t = xs.shape[:2]
    dt = jax.nn.softplus(p[8].astype(jnp.float32).reshape(bsz, t, 2, SSM_HEADS) + dt_bias.astype(jnp.float32))
    return (xs.reshape(bsz, t, SSM_HEADS, SSM_P), bm.reshape(bsz, t, SSM_GROUPS, SSM_N),
            cm.reshape(bsz, t, SSM_GROUPS, SSM_N), dt)


def ssm_post(y, z, norm_w):
    y = y.reshape(*y.shape[:2], SSM_WIDTH) * jax.nn.silu(z)
    return group_rmsnorm(y, norm_w, SSM_GROUPS)


def gla_prep(p, w_gate, b_gate):
    bsz, t = p[9].shape[:2]
    q = p[9].reshape(bsz, t, GLA_HEADS, GLA_DK)
    k = p[10].reshape(bsz, t, GLA_HEADS, GLA_DK)
    v = p[11].reshape(bsz, t, GLA_HEADS, GLA_DV)
    lr = p[13].reshape(bsz, t, 2, GLA_RANK)
    g = jax.nn.log_sigmoid((jnp.einsum('btdr,drk->btdk', lr, w_gate) + b_gate).astype(jnp.float32)) / GLA_GATE_NORM
    return q, k, v, g.reshape(bsz, t, 2, GLA_HEADS, GLA_DK)


def gla_post(o, z, norm_w):
    o = rmsnorm(o, norm_w)
    return o.reshape(*o.shape[:2], GLA_WIDTH) * jax.nn.silu(z)


def merge_branches(o_da, o_ssm, o_gla, p, w_out_da, w_out_ssm, w_out_gla, w_o):
    u = (jax.nn.sigmoid(p[14]) * (o_da @ w_out_da)
         + jax.nn.sigmoid(p[15]) * (o_ssm @ w_out_ssm)
         + jax.nn.sigmoid(p[16]) * (o_gla @ w_out_gla))
    return u @ w_o


def hybrid_layer(x, cx, c, c_ctx, cos, sin, lam_init, need_ctx,
                 w_mod, b_mod, norm_w, w_in, da_lambda, da_norm_w, w_out_da,
                 conv_w, conv_b, dt_bias, a_log, d_skip, ssm_norm_w, w_out_ssm,
                 gla_w_gate, gla_b_gate, gla_norm_w, w_out_gla, w_o):
    f32 = jnp.float32
    bsz, n = x.shape[:2]
    shift, scale, gate = jnp.split((jax.nn.silu(c) @ w_mod + b_mod)[:, None, :], 3, axis=-1)
    shift_c, scale_c, gate_c = jnp.split(jax.nn.silu(c_ctx) @ w_mod + b_mod, 3, axis=-1)
    h = rmsnorm(x, norm_w) * (1.0 + scale) + shift
    hc = rmsnorm(cx, norm_w) * (1.0 + scale_c) + shift_c
    pl = split_cols(h @ w_in, IN_SPLITS)
    pc = split_cols(hc @ w_in, IN_SPLITS)

    q_l, k_l, v_l = da_heads(pl)
    q_l, k_l = apply_rope_2d(q_l, cos, sin), apply_rope_2d(k_l, cos, sin)
    q_c, k_c, v_c = da_heads(pc)
    lam_f = da_lambda.astype(f32)
    lam = jnp.exp(jnp.sum(lam_f[0] * lam_f[1])) - jnp.exp(jnp.sum(lam_f[2] * lam_f[3])) + lam_init
    k_all = jnp.concatenate([k_c, k_l], axis=1)
    v_all = jnp.concatenate([v_c, v_l], axis=1)
    nb = n // Q_BLOCK
    qb = jnp.swapaxes(q_l.reshape(bsz, nb, Q_BLOCK, 2 * DA_HEADS, DA_QK), 0, 1)
    o = lax.map(lambda qi: diff_softmax_attend(qi, k_all, v_all, lam), qb)
    a_lat = da_post(jnp.swapaxes(o, 0, 1).reshape(bsz, n, DA_HEADS, DA_V), pl[3], da_norm_w, lam_init)

    xs_l, b_l, c_l, dt_l = ssm_prep(pl, conv_w, conv_b, dt_bias)
    xs_c, b_c, c_c, dt_c = ssm_prep(pc, conv_w, conv_b, dt_bias)
    a_neg = -jnp.exp(a_log.astype(f32))
    h0_ssm = jnp.zeros((bsz, SSM_GROUPS, SSM_HEADS // SSM_GROUPS, SSM_P, SSM_N), f32)
    ys_c, ys_l = 0.0, 0.0
    for d in range(2):
        scan = functools.partial(ssd_chunked, a=a_neg[d], d_skip=d_skip[d])
        yc, yl = scan_ctx_then_latent(scan, (xs_c, dt_c[:, :, d], b_c, c_c),
                                      (xs_l, dt_l[:, :, d], b_l, c_l), h0_ssm, d == 1)
        ys_c, ys_l = ys_c + yc, ys_l + yl
    s_lat = ssm_post(ys_l, pl[5], ssm_norm_w)

    gq_l, gk_l, gv_l, gg_l = gla_prep(pl, gla_w_gate, gla_b_gate)
    gq_c, gk_c, gv_c, gg_c = gla_prep(pc, gla_w_gate, gla_b_gate)
    h0_gla = jnp.zeros((bsz, GLA_HEADS, GLA_DK, GLA_DV), f32)
    yg_c, yg_l = 0.0, 0.0
    for d in range(2):
        yc, yl = scan_ctx_then_latent(gla_chunked, (gq_c, gk_c, gv_c, gg_c[:, :, d]),
                                      (gq_l, gk_l, gv_l, gg_l[:, :, d]), h0_gla, d == 1)
        yg_c, yg_l = yg_c + yc, yg_l + yl
    g_lat = gla_post(yg_l, pl[12], gla_norm_w)

    x = x + gate * merge_branches(a_lat, s_lat, g_lat, pl, w_out_da, w_out_ssm, w_out_gla, w_o)
    if need_ctx:
        a_ctx = da_post(diff_softmax_attend(q_c, k_c, v_c, lam), pc[3], da_norm_w, lam_init)
        s_ctx = ssm_post(ys_c, pc[5], ssm_norm_w)
        g_ctx = gla_post(yg_c, pc[12], gla_norm_w)
        cx = cx + gate_c * merge_branches(a_ctx, s_ctx, g_ctx, pc, w_out_da, w_out_ssm, w_out_gla, w_o)
    return x, cx


def setup_inputs(seed: int = 0) -> dict:
    key = jax.random.key(seed)
    ks = jax.random.split(key, 26)
    f32 = jnp.float32

    def nrm(k, shape, scale):
        return jax.random.normal(k, shape, f32) * scale

    dt0 = jnp.exp(jax.random.uniform(ks[13], (DEPTH, 2, SSM_HEADS), f32)
                  * (math.log(0.1) - math.log(0.001)) + math.log(0.001))
    return {
        "x": nrm(ks[0], (BATCH, SEQ, D_MODEL), 1.0),
        "c": nrm(ks[1], (BATCH, D_MODEL), 1.0),
        "ctx": nrm(ks[2], (BATCH, CTX_LEN, D_MODEL), 1.0),
        "c_ctx": nrm(ks[3], (D_MODEL,), 1.0),
        "w_mod": nrm(ks[4], (DEPTH, D_MODEL, 3 * D_MODEL), 0.5 * D_MODEL ** -0.5),
        "b_mod": nrm(ks[5], (DEPTH, 3 * D_MODEL), 0.02),
        "norm_w": 1.0 + nrm(ks[6], (DEPTH, D_MODEL), 0.02),
        "w_in": nrm(ks[7], (DEPTH, D_MODEL, IN_WIDTH), D_MODEL ** -0.5),
        "da_lambda": nrm(ks[8], (DEPTH, 4, DA_QK), 0.1),
        "da_norm_w": 1.0 + nrm(ks[9], (DEPTH, DA_V), 0.02),
        "w_out_da": nrm(ks[10], (DEPTH, DA_WIDTH, D_MODEL), DA_WIDTH ** -0.5),
        "ssm_conv_w": nrm(ks[11], (DEPTH, SSM_CONV, SSM_CONV_DIM), SSM_CONV ** -0.5),
        "ssm_conv_b": nrm(ks[12], (DEPTH, SSM_CONV_DIM), 0.02),
        "ssm_dt_bias": dt0 + jnp.log(-jnp.expm1(-dt0)),
        "ssm_a_log": jnp.log(jax.random.uniform(ks[14], (DEPTH, 2, SSM_HEADS), f32, 1.0, 16.0)),
        "ssm_d": 1.0 + nrm(ks[15], (DEPTH, 2, SSM_HEADS), 0.1),
        "ssm_norm_w": 1.0 + nrm(ks[16], (DEPTH, SSM_WIDTH), 0.02),
        "w_out_ssm": nrm(ks[17], (DEPTH, SSM_WIDTH, D_MODEL), SSM_WIDTH ** -0.5),
        "gla_w_gate": nrm(ks[18], (DEPTH, 2, GLA_RANK, GLA_HEADS * GLA_DK), GLA_RANK ** -0.5),
        "gla_b_gate": nrm(ks[19], (DEPTH, 2, GLA_HEADS * GLA_DK), 0.02),
        "gla_norm_w": 1.0 + nrm(ks[20], (DEPTH, GLA_DV), 0.02),
        "w_out_gla": nrm(ks[21], (DEPTH, GLA_WIDTH, D_MODEL), GLA_WIDTH ** -0.5),
        "w_o": nrm(ks[22], (DEPTH, D_MODEL, D_MODEL), D_MODEL ** -0.5),
        "final_norm_w": 1.0 + nrm(ks[23], (D_MODEL,), 0.02),
    }


def reference(x, c, ctx, c_ctx, w_mod, b_mod, norm_w, w_in, da_lambda, da_norm_w, w_out_da,
              ssm_conv_w, ssm_conv_b, ssm_dt_bias, ssm_a_log, ssm_d, ssm_norm_w, w_out_ssm,
              gla_w_gate, gla_b_gate, gla_norm_w, w_out_gla, w_o, final_norm_w):
    n = x.shape[1]
    cos, sin = rope_2d_tables(n, x.dtype)
    cx = ctx
    for l in range(DEPTH):
        lam_init = 0.8 - 0.6 * math.exp(-0.3 * l)
        x, cx = hybrid_layer(x, cx, c, c_ctx, cos, sin, lam_init, l < DEPTH - 1,
                             w_mod[l], b_mod[l], norm_w[l], w_in[l], da_lambda[l], da_norm_w[l], w_out_da[l],
                             ssm_conv_w[l], ssm_conv_b[l], ssm_dt_bias[l], ssm_a_log[l], ssm_d[l],
                             ssm_norm_w[l], w_out_ssm[l],
                             gla_w_gate[l], gla_b_gate[l], gla_norm_w[l], w_out_gla[l], w_o[l])
    return rmsnorm(x, final_norm_w)
```

```python
import functools
import math

import jax
import jax.numpy as jnp
import numpy as np
from jax import lax
from jax.experimental import pallas as pl
from jax.experimental.pallas import tpu as pltpu

F32 = jnp.float32
BF16 = jnp.bfloat16

EPS = 1e-6
GRID_W = 64
ROPE_BASE = 10000.0

DA_HEADS = 4
DA_QK = 64
DA_V = 128
DA_WIDTH = 512
SSM_HEADS = 8
SSM_WIDTH = 512
SSM_GN = 128
GLA_HEADS = 4
GLA_DK = 64
GLA_DV = 128
GLA_WIDTH = 512
GLA_RANK = 16
GLA_GATE_NORM = 16.0
GLA_CHUNK = 64

LANES = 128
TOKEN_BLOCK = 256
MASKED = -1e30
VMEM_LIMIT = 56 * 1024 * 1024

_GROUPS = (("q", 512), ("k", 512), ("v", 512), ("za", 512), ("xbc", 768), ("zs", 512),
           ("gq", 256), ("gk", 256), ("gv", 512), ("zg", 512), ("mg", 3072), ("dt", 128), ("lr", 128))
_F32_GROUPS = ("dt", "lr")
IN_PACKED = sum(w for _, w in _GROUPS)


def _nt(a, b):
    return lax.dot_general(a, b, (((1,), (1,)), ((), ())), preferred_element_type=F32)


def _tn(a, b):
    return lax.dot_general(a, b, (((0,), (0,)), ((), ())), preferred_element_type=F32)


def _nn(a, b):
    return jnp.dot(a, b, preferred_element_type=F32)


def _split3(x):
    hi = x.astype(BF16)
    r1 = x - hi.astype(F32)
    mid = r1.astype(BF16)
    lo = (r1 - mid.astype(F32)).astype(BF16)
    return hi, mid, lo


def _silu(x):
    return x * jax.nn.sigmoid(x)


def _softplus(x):
    return jnp.maximum(x, 0.0) + jnp.log1p(jnp.exp(-jnp.abs(x)))


def _adaln_kernel(c_ref, w_ref, b_ref, o_ref):
    c = c_ref[...]
    o_ref[...] = _nn(_silu(c).astype(BF16), w_ref[...].astype(BF16)) + b_ref[...]


def _adaln(cc, w_mod, b_mod):
    depth, d, d3 = w_mod.shape
    rows = cc.shape[0]
    return pl.pallas_call(
        _adaln_kernel,
        grid=(depth, d3 // d),
        in_specs=[pl.BlockSpec((rows, d), lambda l, j: (0, 0)),
                  pl.BlockSpec((None, d, d), lambda l, j: (l, 0, j)),
                  pl.BlockSpec((None, 1, d), lambda l, j: (l, 0, j))],
        out_specs=pl.BlockSpec((None, rows, d), lambda l, j: (l, 0, j)),
        out_shape=jax.ShapeDtypeStruct((depth, rows, d3), F32),
        compiler_params=pltpu.CompilerParams(dimension_semantics=("parallel", "parallel"),
                                             vmem_limit_bytes=VMEM_LIMIT),
        name="adaln",
    )(cc, w_mod, b_mod.reshape(depth, 1, d3))


def _inproj_kernel(x_ref, mod_ref, nw_ref, cos_ref, sin_ref, w_ref, *out_refs):
    outs = dict(zip([g for g, _ in _GROUPS], out_refs))
    x = x_ref[...]
    ms = jnp.mean(x * x, axis=-1, keepdims=True)
    y = x * lax.rsqrt(ms + EPS) * nw_ref[...]
    h = y * (1.0 + mod_ref[1:2, :]) + mod_ref[0:1, :]
    hb = h.astype(BF16)

    lane = lax.broadcasted_iota(jnp.int32, (1, LANES), 1)
    first_half = (lane % 32) < 16
    cos = cos_ref[...]
    sin = sin_ref[...]

    def rope_store(acc, ref, scale):
        for cb in range(acc.shape[1] // LANES):
            a = acc[:, cb * LANES:(cb + 1) * LANES]
            rot = jnp.where(first_half, -pltpu.roll(a, LANES - 16, 1), pltpu.roll(a, 16, 1))
            ref[:, cb * LANES:(cb + 1) * LANES] = ((a * cos + rot * sin) * scale).astype(ref.dtype)

    off = 0
    for name, width in _GROUPS:
        ref = outs[name]
        for c0 in range(0, width, 512):
            cw = min(512, width - c0)
            acc = _nn(hb, w_ref[:, off + c0:off + c0 + cw])
            if name == "q":
                rope_store(acc, ref, DA_QK ** -0.5)
            elif name == "k":
                rope_store(acc, ref, 1.0)
            elif name == "gq":
                ref[:, c0:c0 + cw] = (acc * (GLA_DK ** -0.5)).astype(ref.dtype)
            else:
                ref[:, c0:c0 + cw] = acc.astype(ref.dtype)
        off += width


def _inproj(x_all, modsel, norm_w, cos, sin, w_all, nl):
    b, t, d = x_all.shape
    tb = TOKEN_BLOCK
    out_shape = [jax.ShapeDtypeStruct((b, t, w), F32 if g in _F32_GROUPS else BF16) for g, w in _GROUPS]
    out_specs = [pl.BlockSpec((None, tb, w), lambda bi, i: (bi, i, 0)) for _, w in _GROUPS]
    return pl.pallas_call(
        _inproj_kernel,
        grid=(b, t // tb),
        in_specs=[pl.BlockSpec((None, tb, d), lambda bi, i: (bi, i, 0)),
                  pl.BlockSpec((None, None, 3, d), lambda bi, i: (bi, i // nl, 0, 0)),
                  pl.BlockSpec((1, d), lambda bi, i: (0, 0)),
                  pl.BlockSpec((tb, LANES), lambda bi, i: (i, 0)),
                  pl.BlockSpec((tb, LANES), lambda bi, i: (i, 0)),
                  pl.BlockSpec((d, IN_PACKED), lambda bi, i: (0, 0), pipeline_mode=pl.Buffered(1))],
        out_specs=out_specs,
        out_shape=out_shape,
        compiler_params=pltpu.CompilerParams(dimension_semantics=("parallel", "parallel"),
                                             vmem_limit_bytes=VMEM_LIMIT),
        name="inproj",
    )(x_all, modsel, norm_w, cos, sin, w_all)


def _attn_kernel(*refs, tq, has_ctx, lam_init):
    if has_ctx:
        lam_ref, q_ref, k_ref, v_ref, kc_ref, vc_ref, z_ref, nw_ref, o_ref, qs, m_s, l_s, acc = refs
    else:
        lam_ref, q_ref, k_ref, v_ref, z_ref, nw_ref, _, o_ref, qs, m_s, l_s, acc = refs
    ki = pl.program_id(3)
    nk = pl.num_programs(3)

    def update(k, v):
        s = _nt(qs[...], k)
        m_prev = m_s[...]
        m_new = jnp.maximum(m_prev, jnp.max(s, axis=-1, keepdims=True))
        alpha = jnp.exp(m_prev - m_new)
        p = jnp.exp(s - m_new)
        l_s[...] = alpha * l_s[...] + jnp.sum(p, axis=-1, keepdims=True)
        acc[...] = alpha * acc[...] + _nn(p.astype(BF16), v)
        m_s[...] = m_new

    @pl.when(ki == 0)
    def _():
        q = q_ref[...]
        lo = lax.broadcasted_iota(jnp.int32, (1, LANES), 1) < DA_QK
        zero = jnp.zeros_like(q)
        qs[0:tq, :] = jnp.where(lo, q, zero)
        qs[tq:2 * tq, :] = jnp.where(lo, zero, q)
        m_s[...] = jnp.full(m_s.shape, MASKED, F32)
        l_s[...] = jnp.zeros(l_s.shape, F32)
        acc[...] = jnp.zeros(acc.shape, F32)
        if has_ctx:
            update(kc_ref[...], vc_ref[...])

    update(k_ref[...], v_ref[...])

    @pl.when(ki == nk - 1)
    def _():
        lp = lam_ref[...]
        lam = (jnp.exp(jnp.sum(lp[0:1, :] * lp[1:2, :], keepdims=True))
               - jnp.exp(jnp.sum(lp[2:3, :] * lp[3:4, :], keepdims=True)) + lam_init)
        a = acc[...]
        l = l_s[...]
        o = a[0:tq, :] / l[0:tq, :] - lam * (a[tq:2 * tq, :] / l[tq:2 * tq, :])
        ms = jnp.mean(o * o, axis=-1, keepdims=True)
        on = o * lax.rsqrt(ms + EPS) * nw_ref[...] * (1.0 - lam_init)
        o_ref[...] = (on * _silu(z_ref[...].astype(F32))).astype(o_ref.dtype)


def _attn_scratch(tq):
    return [pltpu.VMEM((2 * tq, LANES), BF16), pltpu.VMEM((2 * tq, 1), F32),
            pltpu.VMEM((2 * tq, 1), F32), pltpu.VMEM((2 * tq, LANES), F32)]


def _attn_latent(q, k, v, z, da_lambda, da_norm_w, lam_init, n, ct, tq, tk):
    b, t, _ = q.shape
    ctx_blk = n // ct
    blk_q = pl.BlockSpec((None, tq, LANES), lambda bi, h, qi, ki: (bi, qi, h))
    blk_k = pl.BlockSpec((None, tk, LANES), lambda bi, h, qi, ki: (bi, ki, h))
    blk_c = pl.BlockSpec((None, ct, LANES), lambda bi, h, qi, ki: (bi, ctx_blk, h))
    return pl.pallas_call(
        functools.partial(_attn_kernel, tq=tq, has_ctx=True, lam_init=lam_init),
        grid=(b, DA_HEADS, n // tq, n // tk),
        in_specs=[pl.BlockSpec((4, DA_QK), lambda bi, h, qi, ki: (0, 0)),
                  blk_q, blk_k, blk_k, blk_c, blk_c, blk_q,
                  pl.BlockSpec((1, LANES), lambda bi, h, qi, ki: (0, 0))],
        out_specs=blk_q,
        out_shape=jax.ShapeDtypeStruct((b, t, DA_WIDTH), BF16),
        scratch_shapes=_attn_scratch(tq),
        compiler_params=pltpu.CompilerParams(
            dimension_semantics=("parallel", "parallel", "parallel", "arbitrary"),
            vmem_limit_bytes=VMEM_LIMIT),
        name="attn_latent",
    )(da_lambda, q, k, v, k, v, z, da_norm_w)


def _attn_context(q, k, v, z, da_lambda, da_norm_w, lam_init, n, ct, a_all):
    b, t, _ = q.shape
    ctx_blk = n // ct
    blk = pl.BlockSpec((None, ct, LANES), lambda bi, h, qi, ki: (bi, ctx_blk, h))
    return pl.pallas_call(
        functools.partial(_attn_kernel, tq=ct, has_ctx=False, lam_init=lam_init),
        grid=(b, DA_HEADS, 1, 1),
        in_specs=[pl.BlockSpec((4, DA_QK), lambda bi, h, qi, ki: (0, 0)),
                  blk, blk, blk, blk,
                  pl.BlockSpec((1, LANES), lambda bi, h, qi, ki: (0, 0)),
                  pl.BlockSpec(memory_space=pl.ANY)],
        out_specs=blk,
        out_shape=jax.ShapeDtypeStruct((b, t, DA_WIDTH), BF16),
        scratch_shapes=_attn_scratch(ct),
        input_output_aliases={6: 0},
        compiler_params=pltpu.CompilerParams(
            dimension_semantics=("parallel", "parallel", "parallel", "arbitrary"),
            vmem_limit_bytes=VMEM_LIMIT),
        name="attn_context",
    )(da_lambda, q, k, v, z, da_norm_w, a_all)


def _scan_block(s, nl, rev):
    lat = (nl - s) if rev else (s - 1)
    return jnp.where(s == 0, nl, lat)


def _tri_mask(n, rev):
    ri = lax.broadcasted_iota(jnp.int32, (n, n), 0)
    ci = lax.broadcasted_iota(jnp.int32, (n, n), 1)
    return (ri <= ci) if rev else (ri >= ci)


def _ssd_kernel(*refs, rev, post, d, nl):
    if post:
        (xbc_ref, xp_ref, xn_ref, dt_ref, cw_ref, cb_ref, dtb_ref, alog_ref, dsk_ref,
         zs_ref, yo_ref, nw_ref, out_ref, hst) = refs
    else:
        xbc_ref, xp_ref, xn_ref, dt_ref, cw_ref, cb_ref, dtb_ref, alog_ref, dsk_ref, out_ref, hst = refs
    s = pl.program_id(1)
    r = _scan_block(s, nl, rev)
    L = TOKEN_BLOCK

    @pl.when(s == 0)
    def _():
        hst[...] = jnp.zeros(hst.shape, F32)

    xr = xbc_ref[...].astype(F32)
    prev_ok = jnp.logical_and(r != 0, r != nl)
    next_ok = r < nl - 1
    prev_row = jnp.where(prev_ok, xp_ref[7:8, :].astype(F32), 0.0)
    next_row = jnp.where(next_ok, xn_ref[0:1, :].astype(F32), 0.0)
    row = lax.broadcasted_iota(jnp.int32, (L, 1), 0)
    xm1 = jnp.where(row == 0, prev_row, pltpu.roll(xr, 1, 0))
    xp1 = jnp.where(row == L - 1, next_row, pltpu.roll(xr, L - 1, 0))
    conv = cw_ref[0:1, :] * xm1 + cw_ref[1:2, :] * xr + cw_ref[2:3, :] * xp1 + cb_ref[...]
    xbc = _silu(conv)
    xs = xbc[:, 0:SSM_WIDTH]
    bm = xbc[:, SSM_WIDTH:SSM_WIDTH + SSM_GN]
    cm = xbc[:, SSM_WIDTH + SSM_GN:SSM_WIDTH + 2 * SSM_GN]

    lane = lax.broadcasted_iota(jnp.int32, (1, LANES), 1)
    dmask = jnp.logical_and(lane >= SSM_HEADS * d, lane < SSM_HEADS * (d + 1))
    dt = _softplus(dt_ref[...] + dtb_ref[...])
    da = jnp.where(dmask, dt * (-jnp.exp(alog_ref[...])), 0.0)
    tri = _tri_mask(L, rev)
    trib = jnp.where(tri, 1.0, 0.0).astype(BF16)
    hi, mid, lo3 = _split3(da)
    cs = _nn(trib, hi) + _nn(trib, mid) + _nn(trib, lo3)
    tot = cs[0:1, :] if rev else cs[L - 1:L, :]
    cs_t = cs.T
    ecs = jnp.exp(cs)
    est = jnp.exp(tot - cs)
    etot = jnp.exp(tot)

    lo = lane < 64
    cb16 = cm.astype(BF16)
    bb16 = bm.astype(BF16)
    scores = (_nt(jnp.where(lo, cm, 0.0).astype(BF16), bb16),
              _nt(jnp.where(lo, 0.0, cm).astype(BF16), bb16))

    def pair(v, c0):
        return jnp.where(lo, v[:, c0:c0 + 1], v[:, c0 + 1:c0 + 2])

    wts = []
    for h in range(SSM_HEADS):
        c = SSM_HEADS * d + h
        seg = cs[:, c:c + 1] - cs_t[c:c + 1, :]
        wts.append((scores[h // 4] * jnp.exp(jnp.where(tri, seg, MASKED))).astype(BF16))

    rowi = lax.broadcasted_iota(jnp.int32, (LANES, 1), 0)
    ys = []
    for j in range(SSM_HEADS // 2):
        g = j // 2
        c0 = SSM_HEADS * d + 2 * j
        xj = xs[:, j * LANES:(j + 1) * LANES]
        xdt = xj * pair(dt, c0)
        y = (_nn(wts[2 * j], jnp.where(lo, xdt, 0.0).astype(BF16))
             + _nn(wts[2 * j + 1], jnp.where(lo, 0.0, xdt).astype(BF16)))
        hj = hst[j]
        y = y + pair(ecs, c0) * _nn(cb16, hj.astype(BF16))
        y = y + dsk_ref[:, j * LANES:(j + 1) * LANES] * xj
        upd = _tn(bb16, (xdt * pair(est, c0)).astype(BF16))
        in_group = jnp.logical_and(rowi >= 64 * g, rowi < 64 * (g + 1))
        hst[j] = pair(etot, c0) * hj + jnp.where(in_group, upd, 0.0)
        ys.append(y)

    if post:
        yz = []
        for j in range(4):
            z = zs_ref[:, j * LANES:(j + 1) * LANES].astype(F32)
            yz.append((ys[j] + yo_ref[:, j * LANES:(j + 1) * LANES]) * _silu(z))
        for g in range(2):
            ssq = (jnp.sum(yz[2 * g] * yz[2 * g], axis=-1, keepdims=True)
                   + jnp.sum(yz[2 * g + 1] * yz[2 * g + 1], axis=-1, keepdims=True))
            inv = lax.rsqrt(ssq * (1.0 / 256.0) + EPS)
            for j in (2 * g, 2 * g + 1):
                out_ref[:, j * LANES:(j + 1) * LANES] = (
                    yz[j] * inv * nw_ref[:, j * LANES:(j + 1) * LANES]).astype(out_ref.dtype)
    else:
        for j in range(4):
            out_ref[:, j * LANES:(j + 1) * LANES] = ys[j]


def _ssd(xbc, dt, conv_w, conv_b, dtb, alog, dskip, nl, rev, d, post_args=None):
    b, t, cdim = xbc.shape
    tb = TOKEN_BLOCK
    post = post_args is not None
    blk = lambda bi, s: (bi, _scan_block(s, nl, rev), 0)
    halo_p = lambda bi, s: (bi, jnp.maximum(_scan_block(s, nl, rev) * (tb // 8) - 1, 0), 0)
    halo_n = lambda bi, s: (bi, jnp.minimum((_scan_block(s, nl, rev) + 1) * (tb // 8), t // 8 - 1), 0)
    const = lambda bi, s: (0, 0)
    in_specs = [pl.BlockSpec((None, tb, cdim), blk),
                pl.BlockSpec((None, 8, cdim), halo_p),
                pl.BlockSpec((None, 8, cdim), halo_n),
                pl.BlockSpec((None, tb, LANES), blk),
                pl.BlockSpec((3, cdim), const),
                pl.BlockSpec((1, cdim), const),
                pl.BlockSpec((1, LANES), const),
                pl.BlockSpec((1, LANES), const),
                pl.BlockSpec((1, SSM_WIDTH), const)]
    args = [xbc, xbc, xbc, dt, conv_w, conv_b, dtb, alog, dskip]
    if post:
        zs, y_other, norm_w = post_args
        in_specs += [pl.BlockSpec((None, tb, SSM_WIDTH), blk),
                     pl.BlockSpec((None, tb, SSM_WIDTH), blk),
                     pl.BlockSpec((1, SSM_WIDTH), const)]
        args += [zs, y_other, norm_w]
    return pl.pallas_call(
        functools.partial(_ssd_kernel, rev=rev, post=post, d=d, nl=nl),
        grid=(b, nl + 1),
        in_specs=in_specs,
        out_specs=pl.BlockSpec((None, tb, SSM_WIDTH), blk),
        out_shape=jax.ShapeDtypeStruct((b, t, SSM_WIDTH), BF16 if post else F32),
        scratch_shapes=[pltpu.VMEM((SSM_HEADS // 2, LANES, LANES), F32)],
        compiler_params=pltpu.CompilerParams(dimension_semantics=("parallel", "arbitrary"),
                                             vmem_limit_bytes=VMEM_LIMIT),
        name="ssd_bwd" if rev else "ssd_fwd",
    )(*args)


def _gla_kernel(*refs, rev, post, nl):
    if post:
        gq_ref, gk_ref, gv_ref, lr_ref, wg_ref, bg_ref, zg_ref, yo_ref, nw_ref, out_ref, sst = refs
    else:
        gq_ref, gk_ref, gv_ref, lr_ref, wg_ref, bg_ref, out_ref, sst = refs
    s = pl.program_id(1)
    L = TOKEN_BLOCK
    CL = GLA_CHUNK
    kw = GLA_HEADS * GLA_DK

    @pl.when(s == 0)
    def _():
        sst[...] = jnp.zeros(sst.shape, F32)

    pre = _nn(lr_ref[...].astype(BF16), wg_ref[...]) + bg_ref[...]
    g = (jnp.minimum(pre, 0.0) - jnp.log1p(jnp.exp(-jnp.abs(pre)))) * (1.0 / GLA_GATE_NORM)

    tri = _tri_mask(CL, rev)
    trib = jnp.where(tri, 1.0, 0.0).astype(BF16)
    ones = jnp.ones((CL, LANES), BF16)
    lane = lax.broadcasted_iota(jnp.int32, (1, kw), 1)
    rowi = lax.broadcasted_iota(jnp.int32, (kw, 1), 0)
    chunks = range(L // CL)
    for ci in (reversed(chunks) if rev else chunks):
        sl = slice(ci * CL, (ci + 1) * CL)
        hi, mid, lo3 = _split3(g[sl, :])
        gc = _nn(trib, hi) + _nn(trib, mid) + _nn(trib, lo3)
        glast = gc[0:1, :] if rev else gc[CL - 1:CL, :]
        g_tot_col = _tn(hi, ones) + _tn(mid, ones) + _tn(lo3, ones)
        dec = jnp.exp(g_tot_col)
        qf = gq_ref[sl, :].astype(F32)
        kf = gk_ref[sl, :].astype(F32)
        qd = qf * jnp.exp(gc)
        qd16 = qd.astype(BF16)
        kd16 = (kf * jnp.exp(-gc)).astype(BF16)
        kend16 = (kf * jnp.exp(glast - gc)).astype(BF16)
        for h in range(GLA_HEADS):
            hm = jnp.logical_and(lane >= GLA_DK * h, lane < GLA_DK * (h + 1))
            att = jnp.where(tri, _nt(jnp.where(hm, qd, 0.0).astype(BF16), kd16), 0.0)
            vh = gv_ref[sl, h * GLA_DV:(h + 1) * GLA_DV]
            st = sst[h]
            y = _nn(att.astype(BF16), vh) + _nn(qd16, st.astype(BF16))
            in_head = jnp.logical_and(rowi >= GLA_DK * h, rowi < GLA_DK * (h + 1))
            sst[h] = dec * st + jnp.where(in_head, _tn(kend16, vh), 0.0)
            cols = slice(h * GLA_DV, (h + 1) * GLA_DV)
            if post:
                yt = y + yo_ref[sl, cols]
                ms = jnp.mean(yt * yt, axis=-1, keepdims=True)
                o = yt * lax.rsqrt(ms + EPS) * nw_ref[...]
                out_ref[sl, cols] = (o * _silu(zg_ref[sl, cols].astype(F32))).astype(out_ref.dtype)
            else:
                out_ref[sl, cols] = y


def _gla(gq, gk, gv, lr, wg, bg, nl, rev, post_args=None):
    b, t, kw = gq.shape
    tb = TOKEN_BLOCK
    post = post_args is not None
    blk = lambda bi, s: (bi, _scan_block(s, nl, rev), 0)
    const = lambda bi, s: (0, 0)
    in_specs = [pl.BlockSpec((None, tb, kw), blk),
                pl.BlockSpec((None, tb, kw), blk),
                pl.BlockSpec((None, tb, GLA_WIDTH), blk),
                pl.BlockSpec((None, tb, LANES), blk),
                pl.BlockSpec((LANES, kw), const),
                pl.BlockSpec((1, kw), const)]
    args = [gq, gk, gv, lr, wg, bg]
    if post:
        zg, y_other, norm_w = post_args
        in_specs += [pl.BlockSpec((None, tb, GLA_WIDTH), blk),
                     pl.BlockSpec((None, tb, GLA_WIDTH), blk),
                     pl.BlockSpec((1, GLA_DV), const)]
        args += [zg, y_other, norm_w]
    return pl.pallas_call(
        functools.partial(_gla_kernel, rev=rev, post=post, nl=nl),
        grid=(b, nl + 1),
        in_specs=in_specs,
        out_specs=pl.BlockSpec((None, tb, GLA_WIDTH), blk),
        out_shape=jax.ShapeDtypeStruct((b, t, GLA_WIDTH), BF16 if post else F32),
        scratch_shapes=[pltpu.VMEM((GLA_HEADS, kw, GLA_DV), F32)],
        compiler_params=pltpu.CompilerParams(dimension_semantics=("parallel", "arbitrary"),
                                             vmem_limit_bytes=VMEM_LIMIT),
        name="gla_bwd" if rev else "gla_fwd",
    )(*args)


def _merge_kernel(*refs, final):
    if final:
        a_ref, s_ref, g_ref, mg_ref, x_ref, mod_ref, wa_ref, ws_ref, wgl_ref, wo_ref, fw_ref, o_ref = refs
    else:
        a_ref, s_ref, g_ref, mg_ref, x_ref, mod_ref, wa_ref, ws_ref, wgl_ref, wo_ref, o_ref = refs
    d = x_ref.shape[-1]
    u = (jax.nn.sigmoid(mg_ref[:, 0:d].astype(F32)) * _nn(a_ref[...], wa_ref[...])
         + jax.nn.sigmoid(mg_ref[:, d:2 * d].astype(F32)) * _nn(s_ref[...], ws_ref[...])
         + jax.nn.sigmoid(mg_ref[:, 2 * d:3 * d].astype(F32)) * _nn(g_ref[...], wgl_ref[...]))
    xn = x_ref[...] + mod_ref[2:3, :] * _nn(u.astype(BF16), wo_ref[...])
    if final:
        ms = jnp.mean(xn * xn, axis=-1, keepdims=True)
        o_ref[...] = xn * lax.rsqrt(ms + EPS) * fw_ref[...]
    else:
        o_ref[...] = xn


def _merge(a, s, g, mg, x_all, modsel, wa, ws, wgl, wo, nl, final_w=None):
    b, t, d = x_all.shape
    tb = TOKEN_BLOCK
    final = final_w is not None
    nblk = nl if final else t // tb
    blk = lambda bi, i: (bi, i, 0)
    const = lambda bi, i: (0, 0)
    in_specs = [pl.BlockSpec((None, tb, a.shape[-1]), blk),
                pl.BlockSpec((None, tb, s.shape[-1]), blk),
                pl.BlockSpec((None, tb, g.shape[-1]), blk),
                pl.BlockSpec((None, tb, 3 * d), blk),
                pl.BlockSpec((None, tb, d), blk),
                pl.BlockSpec((None, None, 3, d), lambda bi, i: (bi, i // nl, 0, 0)),
                pl.BlockSpec(wa.shape, const),
                pl.BlockSpec(ws.shape, const),
                pl.BlockSpec(wgl.shape, const),
                pl.BlockSpec(wo.shape, const)]
    args = [a, s, g, mg, x_all, modsel, wa, ws, wgl, wo]
    if final:
        in_specs.append(pl.BlockSpec((1, d), const))
        args.append(final_w)
    return pl.pallas_call(
        functools.partial(_merge_kernel, final=final),
        grid=(b, nblk),
        in_specs=in_specs,
        out_specs=pl.BlockSpec((None, tb, d), blk),
        out_shape=jax.ShapeDtypeStruct((b, nblk * tb, d), F32),
        compiler_params=pltpu.CompilerParams(dimension_semantics=("parallel", "parallel"),
                                             vmem_limit_bytes=VMEM_LIMIT),
        name="merge_final" if final else "merge",
    )(*args)


def _rope_tables(n, ct):
    rows = n // GRID_W
    row = jnp.repeat(jnp.arange(rows), GRID_W)
    col = jnp.tile(jnp.arange(GRID_W), rows)
    pos = jnp.stack([row, col], axis=-1).astype(F32)
    nf = DA_QK // 4
    inv = ROPE_BASE ** (-jnp.arange(nf, dtype=F32) / nf)
    ang = jnp.broadcast_to(pos[:, :, None, None] * inv, (n, 2, 2, nf)).reshape(n, DA_QK)
    cos = jnp.concatenate([jnp.cos(ang), jnp.ones((ct, DA_QK), F32)], axis=0)
    sin = jnp.concatenate([jnp.sin(ang), jnp.zeros((ct, DA_QK), F32)], axis=0)
    return jnp.tile(cos, (1, LANES // DA_QK)), jnp.tile(sin, (1, LANES // DA_QK))


def _pack_w_in(w):
    sizes = (512, 512, 512, 512, 512, 512, 128, 128, 16, 256, 256, 512, 512, 32, 1024, 1024, 1024)
    p = jnp.split(w, np.cumsum(sizes)[:-1].tolist(), axis=-1)
    pad = lambda a, n: jnp.pad(a, ((0, 0), (0, n - a.shape[1])))
    cols = [p[0], p[1], p[2], p[3], p[4], p[6], p[7], p[5], p[9], p[10], p[11], p[12],
            p[14], p[15], p[16], pad(p[8], LANES), pad(p[13], LANES)]
    return jnp.concatenate(cols, axis=-1).astype(BF16)


def _lane_row(v):
    v = v.reshape(1, -1).astype(F32)
    return jnp.pad(v, ((0, 0), (0, LANES - v.shape[1])))


def kernel(x, c, ctx, c_ctx, w_mod, b_mod, norm_w, w_in, da_lambda, da_norm_w, w_out_da, ssm_conv_w,
           ssm_conv_b, ssm_dt_bias, ssm_a_log, ssm_d, ssm_norm_w, w_out_ssm, gla_w_gate, gla_b_gate,
           gla_norm_w, w_out_gla, w_o, final_norm_w):
    b, n, d = x.shape
    ct = ctx.shape[1]
    depth = w_mod.shape[0]
    tb = TOKEN_BLOCK
    assert ct == tb and n % tb == 0 and n % GRID_W == 0
    nl = n // tb
    tq = min(512, n)
    tk = min(1024, n)
    kw = GLA_HEADS * GLA_DK

    x_all = jnp.concatenate([x, ctx], axis=1)
    cos, sin = _rope_tables(n, ct)
    rows = 8 * ((b + 1 + 7) // 8)
    cc = jnp.zeros((rows, d), F32).at[:b].set(c).at[b].set(c_ctx)
    mods = _adaln(cc, w_mod, b_mod)

    out = None
    for l in range(depth):
        last = l == depth - 1
        lam_init = 0.8 - 0.6 * math.exp(-0.3 * l)
        m = mods[l].reshape(rows, 3, d)
        modsel = jnp.stack([m[:b], jnp.broadcast_to(m[b], (b, 3, d))], axis=1)
        p = dict(zip([g for g, _ in _GROUPS],
                     _inproj(x_all, modsel, norm_w[l].reshape(1, d), cos, sin, _pack_w_in(w_in[l]), nl)))

        nw_a = da_norm_w[l].reshape(1, DA_V)
        a_all = _attn_latent(p["q"], p["k"], p["v"], p["za"], da_lambda[l], nw_a, lam_init, n, ct, tq, tk)
        if not last:
            a_all = _attn_context(p["q"], p["k"], p["v"], p["za"], da_lambda[l], nw_a, lam_init, n, ct, a_all)

        conv_b = ssm_conv_b[l].reshape(1, -1)
        dtb = _lane_row(ssm_dt_bias[l])
        alog = _lane_row(ssm_a_log[l])
        dsk = [jnp.repeat(ssm_d[l][dd], SSM_WIDTH // SSM_HEADS).reshape(1, SSM_WIDTH) for dd in range(2)]
        y_b = _ssd(p["xbc"], p["dt"], ssm_conv_w[l], conv_b, dtb, alog, dsk[1], nl, True, 1)
        s_all = _ssd(p["xbc"], p["dt"], ssm_conv_w[l], conv_b, dtb, alog, dsk[0], nl, False, 0,
                     (p["zs"], y_b, ssm_norm_w[l].reshape(1, SSM_WIDTH)))

        wg = [jnp.zeros((LANES, kw), F32).at[GLA_RANK * dd:GLA_RANK * (dd + 1)].set(gla_w_gate[l][dd]).astype(BF16)
              for dd in range(2)]
        bg = [gla_b_gate[l][dd].reshape(1, kw) for dd in range(2)]
        yg_b = _gla(p["gq"], p["gk"], p["gv"], p["lr"], wg[1], bg[1], nl, True)
        g_all = _gla(p["gq"], p["gk"], p["gv"], p["lr"], wg[0], bg[0], nl, False,
                     (p["zg"], yg_b, gla_norm_w[l].reshape(1, GLA_DV)))

        res = _merge(a_all, s_all, g_all, p["mg"], x_all, modsel,
                     w_out_da[l].astype(BF16), w_out_ssm[l].astype(BF16), w_out_gla[l].astype(BF16),
                     w_o[l].astype(BF16), nl, final_norm_w.reshape(1, d) if last else None)
        if last:
            out = res
        else:
            x_all = res
    return out
```
